```python
import jax, jax.numpy as jnp
from jax import lax
import numpy as np

D_MODEL = 1024
BATCH = 2
SEQ = 8192
DEPTH = 1

GRID_W = 64
Q_BLOCK = 128
ROPE_THETA = 10000.0
EPS = 1e-6
D_MIX = D_MODEL
A_HEAD_DIM = 64
A_WIDTH = D_MIX // 2
A_HEADS = A_WIDTH // A_HEAD_DIM
A_KV_HEADS = 2
B_WIDTH = D_MIX - A_WIDTH
B_V_HEAD_DIM = 128
B_HEADS = B_WIDTH // B_V_HEAD_DIM
B_NOPE_DIM = 64
B_ROPE_DIM = 32
B_QK_DIM = B_NOPE_DIM + B_ROPE_DIM
B_Q_RANK = 384
B_KV_RANK = 256

_SIZES = (
    A_WIDTH,
    A_KV_HEADS * A_HEAD_DIM,
    A_KV_HEADS * A_HEAD_DIM,
    A_WIDTH,
    B_Q_RANK,
    B_KV_RANK,
    B_ROPE_DIM,
    B_WIDTH,
)
N_IN = int(sum(_SIZES))
_SPLITS = tuple(int(v) for v in np.cumsum(_SIZES)[:-1])

kernel_name = "hybrid_gqa_mla_parallel_groups"


def rms_norm(x, g):
    xf = x.astype(jnp.float32)
    y = xf * lax.rsqrt(jnp.mean(xf * xf, axis=-1, keepdims=True) + EPS)
    return (y * g.astype(jnp.float32)).astype(x.dtype)


def axial_rope_tables(seq_len, dim):
    rows = seq_len // GRID_W
    row = jnp.repeat(jnp.arange(rows, dtype=jnp.float32), GRID_W)
    col = jnp.tile(jnp.arange(GRID_W, dtype=jnp.float32), rows)
    half = dim // 2
    inv = 1.0 / (ROPE_THETA ** (jnp.arange(0, half, 2, dtype=jnp.float32) / half))
    ang_r = row[:, None] * inv[None, :]
    ang_c = col[:, None] * inv[None, :]
    return jnp.cos(ang_r), jnp.sin(ang_r), jnp.cos(ang_c), jnp.sin(ang_c)


def _rotate(xh, cos, sin):
    x1, x2 = jnp.split(xh, 2, axis=-1)
    return jnp.concatenate([x1 * cos - x2 * sin, x1 * sin + x2 * cos], axis=-1)


def apply_axial_rope(x, tables):
    cr, sr, cc, sc = tables
    xf = x.astype(jnp.float32)
    xr, xc = jnp.split(xf, 2, axis=-1)
    out = jnp.concatenate([_rotate(xr, cr, sr), _rotate(xc, cc, sc)], axis=-1)
    return out.astype(x.dtype)


def blocked_attention(q, k, v):
    bsz, n_heads, s_len, dk = q.shape
    n_kv = k.shape[1]
    grp = n_heads // n_kv
    dv = v.shape[-1]
    nb = s_len // Q_BLOCK
    scale = 1.0 / float(np.sqrt(dk))
    qb = q.reshape(bsz, n_kv, grp, nb, Q_BLOCK, dk).transpose(3, 0, 1, 2, 4, 5)

    def one_block(qblk):
        s = jnp.einsum('bkgqd,bksd->bkgqs', qblk, k).astype(jnp.float32) * scale
        p = jax.nn.softmax(s, axis=-1).astype(v.dtype)
        return jnp.einsum('bkgqs,bksd->bkgqd', p, v)

    out = lax.map(one_block, qb)
    out = out.transpose(1, 0, 4, 2, 3, 5)
    return out.reshape(bsz, s_len, n_heads * dv)


def setup_inputs(seed: int = 0) -> dict:
    key = jax.random.key(seed)
    ks = jax.random.split(key, 16)
    f32 = jnp.float32

    def w(k, shape, fan_in):
        return jax.random.normal(k, shape, f32) * (fan_in ** -0.5)

    def gain(k, shape):
        return 1.0 + 0.02 * jax.random.normal(k, shape, f32)

    return {
        "x": jax.random.normal(ks[0], (BATCH, SEQ, D_MODEL), f32),
        "norm_in": gain(ks[1], (DEPTH, D_MODEL)),
        "w_in": w(ks[2], (DEPTH, D_MODEL, N_IN), D_MODEL),
        "a_q_norm": gain(ks[3], (DEPTH, A_HEAD_DIM)),
        "a_k_norm": gain(ks[4], (DEPTH, A_HEAD_DIM)),
        "b_cq_norm": gain(ks[5], (DEPTH, B_Q_RANK)),
        "b_ckv_norm": gain(ks[6], (DEPTH, B_KV_RANK)),
        "w_uq": w(ks[7], (DEPTH, B_Q_RANK, B_HEADS * B_QK_DIM), B_Q_RANK),
        "w_ukv": w(ks[8], (DEPTH, B_KV_RANK, B_HEADS * (B_NOPE_DIM + B_V_HEAD_DIM)), B_KV_RANK),
        "b_q_norm": gain(ks[9], (DEPTH, B_QK_DIM)),
        "b_k_norm": gain(ks[10], (DEPTH, B_QK_DIM)),
        "w_out": w(ks[11], (DEPTH, D_MIX, D_MODEL), D_MIX),
    }


def reference(x, norm_in, w_in, a_q_norm, a_k_norm, b_cq_norm, b_ckv_norm,
              w_uq, w_ukv, b_q_norm, b_k_norm, w_out):
    bsz, s_len, _ = x.shape
    rope_a = axial_rope_tables(s_len, A_HEAD_DIM)
    rope_b = axial_rope_tables(s_len, B_ROPE_DIM)
    h = x
    for l in range(DEPTH):
        xn = rms_norm(h, norm_in[l])
        proj = jnp.einsum('bsd,dn->bsn', xn, w_in[l])
        a_q, a_k, a_v, a_g, b_cq, b_ckv, b_kr, b_g = jnp.split(proj, _SPLITS, axis=-1)

        q = a_q.reshape(bsz, s_len, A_HEADS, A_HEAD_DIM).transpose(0, 2, 1, 3)
        k = a_k.reshape(bsz, s_len, A_KV_HEADS, A_HEAD_DIM).transpose(0, 2, 1, 3)
        v = a_v.reshape(bsz, s_len, A_KV_HEADS, A_HEAD_DIM).transpose(0, 2, 1, 3)
        q = apply_axial_rope(rms_norm(q, a_q_norm[l]), rope_a)
        k = apply_axial_rope(rms_norm(k, a_k_norm[l]), rope_a)
        y_a = blocked_attention(q, k, v)

        c_q = rms_norm(b_cq, b_cq_norm[l])
        c_kv = rms_norm(b_ckv, b_ckv_norm[l])
        qb = jnp.einsum('bsr,rn->bsn', c_q, w_uq[l]).reshape(bsz, s_len, B_HEADS, B_QK_DIM)
        kvb = jnp.einsum('bsr,rn->bsn', c_kv, w_ukv[l]).reshape(
            bsz, s_len, B_HEADS, B_NOPE_DIM + B_V_HEAD_DIM)
        k_nope, vb = jnp.split(kvb, [B_NOPE_DIM], axis=-1)
        k_rope = jnp.broadcast_to(b_kr[:, :, None, :], (bsz, s_len, B_HEADS, B_ROPE_DIM))
        kb = jnp.concatenate([k_nope, k_rope], axis=-1)
        qb = rms_norm(qb, b_q_norm[l]).transpose(0, 2, 1, 3)
        kb = rms_norm(kb, b_k_norm[l]).transpose(0, 2, 1, 3)
        qb = jnp.concatenate([qb[..., :B_NOPE_DIM],
                              apply_axial_rope(qb[..., B_NOPE_DIM:], rope_b)], axis=-1)
        kb = jnp.concatenate([kb[..., :B_NOPE_DIM],
                              apply_axial_rope(kb[..., B_NOPE_DIM:], rope_b)], axis=-1)
        vb = vb.transpose(0, 2, 1, 3)
        y_b = blocked_attention(qb, kb, vb)

        y = jnp.concatenate([y_a * jax.nn.silu(a_g), y_b * jax.nn.silu(b_g)], axis=-1)
        h = h + jnp.einsum('bsm,md->bsd', y, w_out[l])
    return h
```

```python
import functools

import jax
import jax.numpy as jnp
import numpy as np
from jax import lax
from jax.experimental import pallas as pl
from jax.experimental.pallas import tpu as pltpu

GRID_W = 64
ROPE_THETA = 10000.0
EPS = 1e-6
A_HEAD_DIM = 64
A_KV_HEADS = 2
B_V_HEAD_DIM = 128
B_NOPE_DIM = 64
B_ROPE_DIM = 32
B_QK_DIM = B_NOPE_DIM + B_ROPE_DIM
B_Q_RANK = 384
B_KV_RANK = 256

LANES = 128
BF16_SUBLANES = 16
VMEM_LIMIT_BYTES = 48 * 1024 * 1024

BF16 = jnp.bfloat16
F32 = jnp.float32


def _rope_tables_t(seq_len, dim):
    rows = seq_len // GRID_W
    row = jnp.repeat(jnp.arange(rows, dtype=F32), GRID_W)
    col = jnp.tile(jnp.arange(GRID_W, dtype=F32), rows)
    half = dim // 2
    inv = 1.0 / (ROPE_THETA ** (jnp.arange(0, half, 2, dtype=F32) / half))
    ang_r = inv[:, None] * row[None, :]
    ang_c = inv[:, None] * col[None, :]
    return jnp.stack([jnp.cos(ang_r), jnp.sin(ang_r), jnp.cos(ang_c), jnp.sin(ang_c)])


def _rms_rows(xt, gain_col):
    ms = jnp.mean(xt * xt, axis=0, keepdims=True)
    return xt * lax.rsqrt(ms + EPS) * gain_col


def _rope_rows(xt, tab):
    q = xt.shape[0] // 4
    x1r, x2r, x1c, x2c = xt[0:q], xt[q:2 * q], xt[2 * q:3 * q], xt[3 * q:4 * q]
    cr, sr, cc, sc = tab[0], tab[1], tab[2], tab[3]
    return jnp.concatenate(
        [x1r * cr - x2r * sr, x1r * sr + x2r * cr, x1c * cc - x2c * sc, x1c * sc + x2c * cc], axis=0)


def _ones_rows(t):
    r = lax.broadcasted_iota(jnp.int32, (BF16_SUBLANES, t), 0)
    return jnp.where(r == 0, 1.0, 0.0).astype(F32)


def _proj_kernel(x_ref, gin_ref, w_in_ref, gaq_ref, gak_ref, gcq_ref, gckv_ref, w_uq_ref, w_ukv_ref,
                 gbq_ref, gbk_ref, tab_a_ref, tab_b_ref,
                 qa_ref, ka_ref, va_ref, ga_ref, qb_ref, kb_ref, vb_ref, gb_ref,
                 *, a_heads, b_heads, a_scale, b_scale):
    t = x_ref.shape[1]
    a_width = a_heads * A_HEAD_DIM
    b_width = b_heads * B_V_HEAD_DIM
    x = x_ref[0]
    xn = x * lax.rsqrt(jnp.mean(x * x, axis=-1, keepdims=True) + EPS) * gin_ref[...]
    pt = lax.dot_general(w_in_ref[...], xn.astype(BF16), (((1,), (1,)), ((), ())),
                         preferred_element_type=F32)
    o = 0
    aq = pt[o:o + a_width]; o += a_width
    ak = pt[o:o + A_KV_HEADS * A_HEAD_DIM]; o += A_KV_HEADS * A_HEAD_DIM
    av = pt[o:o + A_KV_HEADS * A_HEAD_DIM]; o += A_KV_HEADS * A_HEAD_DIM
    ag = pt[o:o + a_width]; o += a_width
    bcq = pt[o:o + B_Q_RANK]; o += B_Q_RANK
    bckv = pt[o:o + B_KV_RANK]; o += B_KV_RANK
    bkr = pt[o:o + B_ROPE_DIM]; o += B_ROPE_DIM
    bg = pt[o:o + b_width]

    tab_a = tab_a_ref[...]
    tab_b = tab_b_ref[...]
    ones = _ones_rows(t)

    for h in range(a_heads):
        qh = _rms_rows(aq[h * A_HEAD_DIM:(h + 1) * A_HEAD_DIM], gaq_ref[...])
        qa_ref[0, h] = (_rope_rows(qh, tab_a) * a_scale).astype(BF16)
    for h in range(A_KV_HEADS):
        kh = _rope_rows(_rms_rows(ak[h * A_HEAD_DIM:(h + 1) * A_HEAD_DIM], gak_ref[...]), tab_a)
        kh = jnp.concatenate([kh, jnp.zeros((LANES - A_HEAD_DIM, t), F32)], axis=0)
        ka_ref[0, h] = kh.T.astype(BF16)
        vh = av[h * A_HEAD_DIM:(h + 1) * A_HEAD_DIM]
        va_ref[0, h] = jnp.concatenate([vh, ones], axis=0).astype(BF16)
    ga_ref[0] = (ag * jax.nn.sigmoid(ag)).astype(BF16)

    cq = _rms_rows(bcq, gcq_ref[...]).astype(BF16)
    ckv = _rms_rows(bckv, gckv_ref[...]).astype(BF16)
    qbt = jnp.dot(w_uq_ref[...], cq, preferred_element_type=F32)
    kvt = jnp.dot(w_ukv_ref[...], ckv, preferred_element_type=F32)
    kv_dim = B_NOPE_DIM + B_V_HEAD_DIM
    for h in range(b_heads):
        qh = _rms_rows(qbt[h * B_QK_DIM:(h + 1) * B_QK_DIM], gbq_ref[...])
        qh = jnp.concatenate([qh[:B_NOPE_DIM], _rope_rows(qh[B_NOPE_DIM:], tab_b)], axis=0)
        qb_ref[0, h] = (qh * b_scale).astype(BF16)
        kh = jnp.concatenate([kvt[h * kv_dim:h * kv_dim + B_NOPE_DIM], bkr], axis=0)
        kh = _rms_rows(kh, gbk_ref[...])
        kh = jnp.concatenate([kh[:B_NOPE_DIM], _rope_rows(kh[B_NOPE_DIM:], tab_b),
                              jnp.zeros((LANES - B_QK_DIM, t), F32)], axis=0)
        kb_ref[0, h] = kh.T.astype(BF16)
        vh = kvt[h * kv_dim + B_NOPE_DIM:(h + 1) * kv_dim]
        vb_ref[0, h] = jnp.concatenate([vh, ones], axis=0).astype(BF16)
    gb_ref[0] = (bg * jax.nn.sigmoid(bg)).astype(BF16)


def _attn_kernel(q_ref, k_ref, v_ref, g_ref, o_ref, *, dk, dv, tk):
    qt = q_ref[0, 0]
    tq = qt.shape[1]
    nk = k_ref.shape[2] // tk

    def body(j, carry):
        m, acc = carry
        start = pl.multiple_of(j * tk, tk)
        k = k_ref[0, 0, pl.ds(start, tk), :][:, :dk]
        s = jnp.dot(k, qt, preferred_element_type=F32)
        m_new = jnp.maximum(m, jnp.max(s, axis=0, keepdims=True))
        p = jnp.exp(s - m_new).astype(BF16)
        alpha = jnp.exp(m - m_new)
        vt = v_ref[0, 0, :, pl.ds(start, tk)]
        acc = acc * alpha + jnp.dot(vt, p, preferred_element_type=F32)
        return m_new, acc

    m0 = jnp.full((1, tq), -jnp.inf, F32)
    acc0 = jnp.zeros((dv + BF16_SUBLANES, tq), F32)
    _, acc = lax.fori_loop(0, nk, body, (m0, acc0))
    denom = acc[dv:dv + 1]
    y = acc[:dv] * (1.0 / denom)
    o_ref[0] = (y * g_ref[0].astype(F32)).astype(BF16)


def _out_kernel(x_ref, ya_ref, yb_ref, wa_ref, wb_ref, o_ref):
    dn = (((0,), (0,)), ((), ()))
    upd = lax.dot_general(ya_ref[0], wa_ref[...], dn, preferred_element_type=F32)
    upd = upd + lax.dot_general(yb_ref[0], wb_ref[...], dn, preferred_element_type=F32)
    o_ref[0] = x_ref[0] + upd


def _attention(qt, k, vt, gt, *, dk, dv, group, tq, tk):
    bsz, n_heads, _, s_len = qt.shape
    n_kv = k.shape[1]
    assert n_heads == n_kv * group and s_len % tq == 0 and s_len % tk == 0
    grid = (bsz, n_kv, group, s_len // tq)
    return pl.pallas_call(
        functools.partial(_attn_kernel, dk=dk, dv=dv, tk=tk),
        grid=grid,
        in_specs=[
            pl.BlockSpec((1, 1, dk, tq), lambda b, kv, g, i: (b, kv * group + g, 0, i)),
            pl.BlockSpec((1, 1, s_len, LANES), lambda b, kv, g, i: (b, kv, 0, 0)),
            pl.BlockSpec((1, 1, dv + BF16_SUBLANES, s_len), lambda b, kv, g, i: (b, kv, 0, 0)),
            pl.BlockSpec((1, dv, tq), lambda b, kv, g, i: (b, kv * group + g, i)),
        ],
        out_specs=pl.BlockSpec((1, dv, tq), lambda b, kv, g, i: (b, kv * group + g, i)),
        out_shape=jax.ShapeDtypeStruct((bsz, n_heads * dv, s_len), BF16),
        compiler_params=pltpu.CompilerParams(
            dimension_semantics=("arbitrary",) * 4, vmem_limit_bytes=VMEM_LIMIT_BYTES),
    )(qt, k, vt, gt)


def _layer(h, norm_in, w_in, a_q_norm, a_k_norm, b_cq_norm, b_ckv_norm, w_uq, w_ukv, b_q_norm, b_k_norm,
           w_out, tab_a, tab_b):
    bsz, s_len, d_model = h.shape
    n_in = w_in.shape[1]
    b_heads = w_uq.shape[1] // B_QK_DIM
    b_width = b_heads * B_V_HEAD_DIM
    a_width = w_out.shape[0] - b_width
    a_heads = a_width // A_HEAD_DIM
    group = a_heads // A_KV_HEADS
    a_scale = 1.0 / float(np.sqrt(A_HEAD_DIM))
    b_scale = 1.0 / float(np.sqrt(B_QK_DIM))

    tp = min(256, s_len)
    tq = min(256, s_len)
    tk = min(512, s_len)
    to = min(256, s_len)

    col = lambda g: g.reshape(-1, 1).astype(F32)
    full = lambda a: pl.BlockSpec(a.shape, lambda b, i: (0,) * a.ndim)
    w_in_t = w_in.T.astype(BF16)
    w_uq_t = w_uq.T.astype(BF16)
    w_ukv_t = w_ukv.T.astype(BF16)
    small = [norm_in.reshape(1, -1).astype(F32), w_in_t, col(a_q_norm), col(a_k_norm), col(b_cq_norm),
             col(b_ckv_norm), w_uq_t, w_ukv_t, col(b_q_norm), col(b_k_norm)]

    va_rows = A_HEAD_DIM + BF16_SUBLANES
    vb_rows = B_V_HEAD_DIM + BF16_SUBLANES
    out_shapes = [
        jax.ShapeDtypeStruct((bsz, a_heads, A_HEAD_DIM, s_len), BF16),
        jax.ShapeDtypeStruct((bsz, A_KV_HEADS, s_len, LANES), BF16),
        jax.ShapeDtypeStruct((bsz, A_KV_HEADS, va_rows, s_len), BF16),
        jax.ShapeDtypeStruct((bsz, a_width, s_len), BF16),
        jax.ShapeDtypeStruct((bsz, b_heads, B_QK_DIM, s_len), BF16),
        jax.ShapeDtypeStruct((bsz, b_heads, s_len, LANES), BF16),
        jax.ShapeDtypeStruct((bsz, b_heads, vb_rows, s_len), BF16),
        jax.ShapeDtypeStruct((bsz, b_width, s_len), BF16),
    ]
    out_specs = [
        pl.BlockSpec((1, a_heads, A_HEAD_DIM, tp), lambda b, i: (b, 0, 0, i)),
        pl.BlockSpec((1, A_KV_HEADS, tp, LANES), lambda b, i: (b, 0, i, 0)),
        pl.BlockSpec((1, A_KV_HEADS, va_rows, tp), lambda b, i: (b, 0, 0, i)),
        pl.BlockSpec((1, a_width, tp), lambda b, i: (b, 0, i)),
        pl.BlockSpec((1, b_heads, B_QK_DIM, tp), lambda b, i: (b, 0, 0, i)),
        pl.BlockSpec((1, b_heads, tp, LANES), lambda b, i: (b, 0, i, 0)),
        pl.BlockSpec((1, b_heads, vb_rows, tp), lambda b, i: (b, 0, 0, i)),
        pl.BlockSpec((1, b_width, tp), lambda b, i: (b, 0, i)),
    ]
    qa, ka, va, ga, qb, kb, vb, gb = pl.pallas_call(
        functools.partial(_proj_kernel, a_heads=a_heads, b_heads=b_heads, a_scale=a_scale, b_scale=b_scale),
        grid=(bsz, s_len // tp),
        in_specs=[pl.BlockSpec((1, tp, d_model), lambda b, i: (b, i, 0))] + [full(a) for a in small] + [
            pl.BlockSpec((4, A_HEAD_DIM // 4, tp), lambda b, i: (0, 0, i)),
            pl.BlockSpec((4, B_ROPE_DIM // 4, tp), lambda b, i: (0, 0, i)),
        ],
        out_specs=out_specs,
        out_shape=out_shapes,
        compiler_params=pltpu.CompilerParams(
            dimension_semantics=("arbitrary", "arbitrary"), vmem_limit_bytes=VMEM_LIMIT_BYTES),
    )(h, *small, tab_a, tab_b)
    assert n_in == w_in_t.shape[0]

    ya = _attention(qa, ka, va, ga, dk=A_HEAD_DIM, dv=A_HEAD_DIM, group=group, tq=tq, tk=tk)
    yb = _attention(qb, kb, vb, gb, dk=B_QK_DIM, dv=B_V_HEAD_DIM, group=1, tq=tq, tk=tk)

    wa = w_out[:a_width].astype(BF16)
    wb = w_out[a_width:].astype(BF16)
    return pl.pallas_call(
        _out_kernel,
        grid=(bsz, s_len // to),
        in_specs=[
            pl.BlockSpec((1, to, d_model), lambda b, i: (b, i, 0)),
            pl.BlockSpec((1, a_width, to), lambda b, i: (b, 0, i)),
            pl.BlockSpec((1, b_width, to), lambda b, i: (b, 0, i)),
            pl.BlockSpec(wa.shape, lambda b, i: (0, 0)),
            pl.BlockSpec(wb.shape, lambda b, i: (0, 0)),
        ],
        out_specs=pl.BlockSpec((1, to, d_model), lambda b, i: (b, i, 0)),
        out_shape=jax.ShapeDtypeStruct(h.shape, h.dtype),
        compiler_params=pltpu.CompilerParams(
            dimension_semantics=("arbitrary", "arbitrary"), vmem_limit_bytes=VMEM_LIMIT_BYTES),
    )(h, ya, yb, wa, wb)


def kernel(x, norm_in, w_in, a_q_norm, a_k_norm, b_cq_norm, b_ckv_norm, w_uq, w_ukv, b_q_norm, b_k_norm, w_out):
    s_len = x.shape[1]
    tab_a = _rope_tables_t(s_len, A_HEAD_DIM)
    tab_b = _rope_tables_t(s_len, B_ROPE_DIM)
    h = x
    for l in range(norm_in.shape[0]):
        h = _layer(h, norm_in[l], w_in[l], a_q_norm[l], a_k_norm[l], b_cq_norm[l], b_ckv_norm[l],
                   w_uq[l], w_ukv[l], b_q_norm[l], b_k_norm[l], w_out[l], tab_a, tab_b)
    return h
```

```python
import functools

import jax
import jax.numpy as jnp
import numpy as np
from jax import lax
from jax.experimental import pallas as pl
from jax.experimental.pallas import tpu as pltpu

GRID_W = 64
ROPE_THETA = 10000.0
EPS = 1e-6
A_HEAD_DIM = 64
A_KV_HEADS = 2
B_V_HEAD_DIM = 128
B_NOPE_DIM = 64
B_ROPE_DIM = 32
B_QK_DIM = B_NOPE_DIM + B_ROPE_DIM
B_Q_RANK = 384
B_KV_RANK = 256

LANES = 128
BF16_SUBLANES = 16
VMEM_LIMIT_BYTES = 48 * 1024 * 1024
LOG2_E = float(np.log2(np.e))

BF16 = jnp.bfloat16
F32 = jnp.float32


def _rope_tables_t(seq_len, dim):
    rows = seq_len // GRID_W
    row = jnp.repeat(jnp.arange(rows, dtype=F32), GRID_W)
    col = jnp.tile(jnp.arange(GRID_W, dtype=F32), rows)
    half = dim // 2
    inv = 1.0 / (ROPE_THETA ** (jnp.arange(0, half, 2, dtype=F32) / half))
    ang_r = inv[:, None] * row[None, :]
    ang_c = inv[:, None] * col[None, :]
    return jnp.stack([jnp.cos(ang_r), jnp.sin(ang_r), jnp.cos(ang_c), jnp.sin(ang_c)])


def _rms_rows(xt, gain_col):
    ms = jnp.mean(xt * xt, axis=0, keepdims=True)
    return xt * lax.rsqrt(ms + EPS) * gain_col


def _rope_rows(xt, tab):
    q = xt.shape[0] // 4
    x1r, x2r, x1c, x2c = xt[0:q], xt[q:2 * q], xt[2 * q:3 * q], xt[3 * q:4 * q]
    cr, sr, cc, sc = tab[0], tab[1], tab[2], tab[3]
    return jnp.concatenate(
        [x1r * cr - x2r * sr, x1r * sr + x2r * cr, x1c * cc - x2c * sc, x1c * sc + x2c * cc], axis=0)


def _ones_rows(t):
    r = lax.broadcasted_iota(jnp.int32, (BF16_SUBLANES, t), 0)
    return jnp.where(r == 0, 1.0, 0.0).astype(F32)


def _proj_kernel(x_ref, gin_ref, w_in_ref, gaq_ref, gak_ref, gcq_ref, gckv_ref, w_uq_ref, w_ukv_ref,
                 gbq_ref, gbk_ref, tab_a_ref, tab_b_ref,
                 qa_ref, ka_ref, va_ref, ga_ref, qb_ref, kb_ref, vb_ref, gb_ref,
                 *, a_heads, b_heads, a_scale, b_scale):
    t = x_ref.shape[1]
    a_width = a_heads * A_HEAD_DIM
    b_width = b_heads * B_V_HEAD_DIM
    x = x_ref[0]
    xn = x * lax.rsqrt(jnp.mean(x * x, axis=-1, keepdims=True) + EPS) * gin_ref[...]
    pt = lax.dot_general(w_in_ref[...], xn.astype(BF16), (((1,), (1,)), ((), ())),
                         preferred_element_type=F32)
    o = 0
    aq = pt[o:o + a_width]; o += a_width
    ak = pt[o:o + A_KV_HEADS * A_HEAD_DIM]; o += A_KV_HEADS * A_HEAD_DIM
    av = pt[o:o + A_KV_HEADS * A_HEAD_DIM]; o += A_KV_HEADS * A_HEAD_DIM
    ag = pt[o:o + a_width]; o += a_width
    bcq = pt[o:o + B_Q_RANK]; o += B_Q_RANK
    bckv = pt[o:o + B_KV_RANK]; o += B_KV_RANK
    bkr = pt[o:o + B_ROPE_DIM]; o += B_ROPE_DIM
    bg = pt[o:o + b_width]

    tab_a = tab_a_ref[...]
    tab_b = tab_b_ref[...]
    ones = _ones_rows(t)

    for h in range(a_heads):
        qh = _rms_rows(aq[h * A_HEAD_DIM:(h + 1) * A_HEAD_DIM], gaq_ref[...])
        qa_ref[0, h] = (_rope_rows(qh, tab_a) * a_scale).astype(BF16)
    for h in range(A_KV_HEADS):
        kh = _rope_rows(_rms_rows(ak[h * A_HEAD_DIM:(h + 1) * A_HEAD_DIM], gak_ref[...]), tab_a)
        kh = jnp.concatenate([kh, jnp.zeros((LANES - A_HEAD_DIM, t), F32)], axis=0)
        ka_ref[0, h] = kh.T.astype(BF16)
        vh = av[h * A_HEAD_DIM:(h + 1) * A_HEAD_DIM]
        va_ref[0, h] = jnp.concatenate([vh, ones], axis=0).astype(BF16)
    ga_ref[0] = (ag * jax.nn.sigmoid(ag)).astype(BF16)

    cq = _rms_rows(bcq, gcq_ref[...]).astype(BF16)
    ckv = _rms_rows(bckv, gckv_ref[...]).astype(BF16)
    qbt = jnp.dot(w_uq_ref[...], cq, preferred_element_type=F32)
    kvt = jnp.dot(w_ukv_ref[...], ckv, preferred_element_type=F32)
    kv_dim = B_NOPE_DIM + B_V_HEAD_DIM
    for h in range(b_heads):
        qh = _rms_rows(qbt[h * B_QK_DIM:(h + 1) * B_QK_DIM], gbq_ref[...])
        qh = jnp.concatenate([qh[:B_NOPE_DIM], _rope_rows(qh[B_NOPE_DIM:], tab_b)], axis=0)
        qb_ref[0, h] = (qh * b_scale).astype(BF16)
        kh = jnp.concatenate([kvt[h * kv_dim:h * kv_dim + B_NOPE_DIM], bkr], axis=0)
        kh = _rms_rows(kh, gbk_ref[...])
        kh = jnp.concatenate([kh[:B_NOPE_DIM], _rope_rows(kh[B_NOPE_DIM:], tab_b),
                              jnp.zeros((LANES - B_QK_DIM, t), F32)], axis=0)
        kb_ref[0, h] = kh.T.astype(BF16)
        vh = kvt[h * kv_dim + B_NOPE_DIM:(h + 1) * kv_dim]
        vb_ref[0, h] = jnp.concatenate([vh, ones], axis=0).astype(BF16)
    gb_ref[0] = (bg * jax.nn.sigmoid(bg)).astype(BF16)


def _attn_kernel(q_ref, k_ref, v_ref, g_ref, o_ref, s_ref, cmax_ref, m_ref, acc_ref, *, dk, dv, tk, ts):
    tq = q_ref.shape[3]
    nk = k_ref.shape[2] // tk
    strips = [slice(c * ts, (c + 1) * ts) for c in range(tq // ts)]
    m_ref[...] = jnp.full(m_ref.shape, -jnp.inf, F32)
    acc_ref[...] = jnp.zeros(acc_ref.shape, F32)

    def scores(j, buf, cols):
        start = pl.multiple_of(j * tk, tk)
        k = k_ref[0, 0, pl.ds(start, tk), :][:, :dk]
        s = jnp.dot(k, q_ref[0, 0, :, cols], preferred_element_type=F32)
        s_ref[buf, :, cols] = s
        cmax_ref[buf, :, cols] = jnp.max(s, axis=0, keepdims=True)

    def update(j, buf, cols):
        start = pl.multiple_of(j * tk, tk)
        vt = v_ref[0, 0, :, pl.ds(start, tk)]
        m = m_ref[:, cols]
        m_new = jnp.maximum(m, cmax_ref[buf, :, cols])
        p = jnp.exp2((s_ref[buf, :, cols] - m_new).astype(BF16))
        alpha = jnp.exp2(m - m_new)
        m_ref[:, cols] = m_new
        acc_ref[:, cols] = acc_ref[:, cols] * alpha + jnp.dot(vt, p, preferred_element_type=F32)

    def step(j_next, j_cur, buf_cur):
        for cols in strips:
            if j_next is not None:
                scores(j_next, 1 - buf_cur, cols)
            update(j_cur, buf_cur, cols)

    for cols in strips:
        scores(0, 0, cols)

    def body(i, _):
        j = 2 * i
        step(j + 1, j, 0)
        step(j + 2, j + 1, 1)
        return 0

    lax.fori_loop(0, nk // 2 - 1, body, 0)
    step(nk - 1, nk - 2, 0)
    step(None, nk - 1, 1)

    denom = acc_ref[dv:dv + 1, :]
    y = acc_ref[0:dv, :] * (1.0 / denom)
    o_ref[0] = (y * g_ref[0].astype(F32)).astype(BF16)


def _out_kernel(x_ref, ya_ref, yb_ref, wa_ref, wb_ref, o_ref):
    dn = (((0,), (0,)), ((), ()))
    upd = lax.dot_general(ya_ref[0], wa_ref[...], dn, preferred_element_type=F32)
    upd = upd + lax.dot_general(yb_ref[0], wb_ref[...], dn, preferred_element_type=F32)
    o_ref[0] = x_ref[0] + upd


def _attention(qt, k, vt, gt, *, dk, dv, group, tq, ts, tk):
    bsz, n_heads, _, s_len = qt.shape
    n_kv = k.shape[1]
    assert n_heads == n_kv * group and s_len % tq == 0 and tq % ts == 0
    assert s_len % (2 * tk) == 0, "the key-chunk pipeline runs two chunks per step"
    grid = (bsz, n_kv, group, s_len // tq)
    return pl.pallas_call(
        functools.partial(_attn_kernel, dk=dk, dv=dv, tk=tk, ts=ts),
        scratch_shapes=[pltpu.VMEM((2, tk, tq), F32), pltpu.VMEM((2, 1, tq), F32),
                        pltpu.VMEM((1, tq), F32), pltpu.VMEM((dv + BF16_SUBLANES, tq), F32)],
        grid=grid,
        in_specs=[
            pl.BlockSpec((1, 1, dk, tq), lambda b, kv, g, i: (b, kv * group + g, 0, i)),
            pl.BlockSpec((1, 1, s_len, LANES), lambda b, kv, g, i: (b, kv, 0, 0)),
            pl.BlockSpec((1, 1, dv + BF16_SUBLANES, s_len), lambda b, kv, g, i: (b, kv, 0, 0)),
            pl.BlockSpec((1, dv, tq), lambda b, kv, g, i: (b, kv * group + g, i)),
        ],
        out_specs=pl.BlockSpec((1, dv, tq), lambda b, kv, g, i: (b, kv * group + g, i)),
        out_shape=jax.ShapeDtypeStruct((bsz, n_heads * dv, s_len), BF16),
        compiler_params=pltpu.CompilerParams(
            dimension_semantics=("arbitrary",) * 4, vmem_limit_bytes=VMEM_LIMIT_BYTES),
    )(qt, k, vt, gt)


def _layer(h, norm_in, w_in, a_q_norm, a_k_norm, b_cq_norm, b_ckv_norm, w_uq, w_ukv, b_q_norm, b_k_norm,
           w_out, tab_a, tab_b):
    bsz, s_len, d_model = h.shape
    n_in = w_in.shape[1]
    b_heads = w_uq.shape[1] // B_QK_DIM
    b_width = b_heads * B_V_HEAD_DIM
    a_width = w_out.shape[0] - b_width
    a_heads = a_width // A_HEAD_DIM
    group = a_heads // A_KV_HEADS
    a_scale = LOG2_E / float(np.sqrt(A_HEAD_DIM))
    b_scale = LOG2_E / float(np.sqrt(B_QK_DIM))

    tp = min(256, s_len)
    tq = min(1024, s_len)
    ts = min(256, s_len)
    tk = min(512, s_len // 2)
    to = min(256, s_len)

    col = lambda g: g.reshape(-1, 1).astype(F32)
    full = lambda a: pl.BlockSpec(a.shape, lambda b, i: (0,) * a.ndim)
    w_in_t = w_in.T.astype(BF16)
    w_uq_t = w_uq.T.astype(BF16)
    w_ukv_t = w_ukv.T.astype(BF16)
    small = [norm_in.reshape(1, -1).astype(F32), w_in_t, col(a_q_norm), col(a_k_norm), col(b_cq_norm),
             col(b_ckv_norm), w_uq_t, w_ukv_t, col(b_q_norm), col(b_k_norm)]

    va_rows = A_HEAD_DIM + BF16_SUBLANES
    vb_rows = B_V_HEAD_DIM + BF16_SUBLANES
    out_shapes = [
        jax.ShapeDtypeStruct((bsz, a_heads, A_HEAD_DIM, s_len), BF16),
        jax.ShapeDtypeStruct((bsz, A_KV_HEADS, s_len, LANES), BF16),
        jax.ShapeDtypeStruct((bsz, A_KV_HEADS, va_rows, s_len), BF16),
        jax.ShapeDtypeStruct((bsz, a_width, s_len), BF16),
        jax.ShapeDtypeStruct((bsz, b_heads, B_QK_DIM, s_len), BF16),
        jax.ShapeDtypeStruct((bsz, b_heads, s_len, LANES), BF16),
        jax.ShapeDtypeStruct((bsz, b_heads, vb_rows, s_len), BF16),
        jax.ShapeDtypeStruct((bsz, b_width, s_len), BF16),
    ]
    out_specs = [
        pl.BlockSpec((1, a_heads, A_HEAD_DIM, tp), lambda b, i: (b, 0, 0, i)),
        pl.BlockSpec((1, A_KV_HEADS, tp, LANES), lambda b, i: (b, 0, i, 0)),
        pl.BlockSpec((1, A_KV_HEADS, va_rows, tp), lambda b, i: (b, 0, 0, i)),
        pl.BlockSpec((1, a_width, tp), lambda b, i: (b, 0, i)),
        pl.BlockSpec((1, b_heads, B_QK_DIM, tp), lambda b, i: (b, 0, 0, i)),
        pl.BlockSpec((1, b_heads, tp, LANES), lambda b, i: (b, 0, i, 0)),
        pl.BlockSpec((1, b_heads, vb_rows, tp), lambda b, i: (b, 0, 0, i)),
        pl.BlockSpec((1, b_width, tp), lambda b, i: (b, 0, i)),
    ]
    qa, ka, va, ga, qb, kb, vb, gb = pl.pallas_call(
        functools.partial(_proj_kernel, a_heads=a_heads, b_heads=b_heads, a_scale=a_scale, b_scale=b_scale),
        grid=(bsz, s_len // tp),
        in_specs=[pl.BlockSpec((1, tp, d_model), lambda b, i: (b, i, 0))] + [full(a) for a in small] + [
            pl.BlockSpec((4, A_HEAD_DIM // 4, tp), lambda b, i: (0, 0, i)),
            pl.BlockSpec((4, B_ROPE_DIM // 4, tp), lambda b, i: (0, 0, i)),
        ],
        out_specs=out_specs,
        out_shape=out_shapes,
        compiler_params=pltpu.CompilerParams(
            dimension_semantics=("arbitrary", "arbitrary"), vmem_limit_bytes=VMEM_LIMIT_BYTES),
    )(h, *small, tab_a, tab_b)
    assert n_in == w_in_t.shape[0]

    ya = _attention(qa, ka, va, ga, dk=A_HEAD_DIM, dv=A_HEAD_DIM, group=group, tq=tq, ts=ts, tk=tk)
    yb = _attention(qb, kb, vb, gb, dk=B_QK_DIM, dv=B_V_HEAD_DIM, group=1, tq=tq, ts=ts, tk=tk)

    wa = w_out[:a_width].astype(BF16)
    wb = w_out[a_width:].astype(BF16)
    return pl.pallas_call(
        _out_kernel,
        grid=(bsz, s_len // to),
        in_specs=[
            pl.BlockSpec((1, to, d_model), lambda b, i: (b, i, 0)),
            pl.BlockSpec((1, a_width, to), lambda b, i: (b, 0, i)),
            pl.BlockSpec((1, b_width, to), lambda b, i: (b, 0, i)),
            pl.BlockSpec(wa.shape, lambda b, i: (0, 0)),
            pl.BlockSpec(wb.shape, lambda b, i: (0, 0)),
        ],
        out_specs=pl.BlockSpec((1, to, d_model), lambda b, i: (b, i, 0)),
        out_shape=jax.ShapeDtypeStruct(h.shape, h.dtype),
        compiler_params=pltpu.CompilerParams(
            dimension_semantics=("arbitrary", "arbitrary"), vmem_limit_bytes=VMEM_LIMIT_BYTES),
    )(h, ya, yb, wa, wb)


def kernel(x, norm_in, w_in, a_q_norm, a_k_norm, b_cq_norm, b_ckv_norm, w_uq, w_ukv, b_q_norm, b_k_norm, w_out):
    s_len = x.shape[1]
    tab_a = _rope_tables_t(s_len, A_HEAD_DIM)
    tab_b = _rope_tables_t(s_len, B_ROPE_DIM)
    h = x
    for l in range(norm_in.shape[0]):
        h = _layer(h, norm_in[l], w_in[l], a_q_norm[l], a_k_norm[l], b_cq_norm[l], b_ckv_norm[l],
                   w_uq[l], w_ukv[l], b_q_norm[l], b_k_norm[l], w_out[l], tab_a, tab_b)
    return h
```

```python
import functools

import jax
import jax.numpy as jnp
import numpy as np
from jax import lax
from jax.experimental import pallas as pl
from jax.experimental.pallas import tpu as pltpu

GRID_W = 64
ROPE_THETA = 10000.0
EPS = 1e-6
A_HEAD_DIM = 64
A_KV_HEADS = 2
B_V_HEAD_DIM = 128
B_NOPE_DIM = 64
B_ROPE_DIM = 32
B_QK_DIM = B_NOPE_DIM + B_ROPE_DIM
B_Q_RANK = 384
B_KV_RANK = 256

LANES = 128
BF16_SUBLANES = 16
VMEM_LIMIT_BYTES = 48 * 1024 * 1024
LOG2_E = float(np.log2(np.e))

BF16 = jnp.bfloat16
F32 = jnp.float32


def _rope_tables_t(seq_len, dim):
    rows = seq_len // GRID_W
    row = jnp.repeat(jnp.arange(rows, dtype=F32), GRID_W)
    col = jnp.tile(jnp.arange(GRID_W, dtype=F32), rows)
    half = dim // 2
    inv = 1.0 / (ROPE_THETA ** (jnp.arange(0, half, 2, dtype=F32) / half))
    ang_r = inv[:, None] * row[None, :]
    ang_c = inv[:, None] * col[None, :]
    return jnp.stack([jnp.cos(ang_r), jnp.sin(ang_r), jnp.cos(ang_c), jnp.sin(ang_c)])


def _rms_rows(xt, gain_col):
    ms = jnp.mean(xt * xt, axis=0, keepdims=True)
    return xt * lax.rsqrt(ms + EPS) * gain_col


def _rope_rows(xt, tab):
    q = xt.shape[0] // 4
    x1r, x2r, x1c, x2c = xt[0:q], xt[q:2 * q], xt[2 * q:3 * q], xt[3 * q:4 * q]
    cr, sr, cc, sc = tab[0], tab[1], tab[2], tab[3]
    return jnp.concatenate(
        [x1r * cr - x2r * sr, x1r * sr + x2r * cr, x1c * cc - x2c * sc, x1c * sc + x2c * cc], axis=0)


def _ones_rows(t):
    r = lax.broadcasted_iota(jnp.int32, (BF16_SUBLANES, t), 0)
    return jnp.where(r == 0, 1.0, 0.0).astype(F32)


def _proj_kernel(x_ref, gin_ref, w_in_ref, gaq_ref, gak_ref, gcq_ref, gckv_ref, w_uq_ref, w_ukv_ref,
                 gbq_ref, gbk_ref, tab_a_ref, tab_b_ref,
                 qa_ref, ka_ref, va_ref, ga_ref, qb_ref, kb_ref, vb_ref, gb_ref,
                 *, a_heads, b_heads, a_scale, b_scale):
    t = x_ref.shape[1]
    a_width = a_heads * A_HEAD_DIM
    b_width = b_heads * B_V_HEAD_DIM
    x = x_ref[0]
    xn = x * lax.rsqrt(jnp.mean(x * x, axis=-1, keepdims=True) + EPS) * gin_ref[...]
    pt = lax.dot_general(w_in_ref[...], xn.astype(BF16), (((1,), (1,)), ((), ())),
                         preferred_element_type=F32)
    o = 0
    aq = pt[o:o + a_width]; o += a_width
    ak = pt[o:o + A_KV_HEADS * A_HEAD_DIM]; o += A_KV_HEADS * A_HEAD_DIM
    av = pt[o:o + A_KV_HEADS * A_HEAD_DIM]; o += A_KV_HEADS * A_HEAD_DIM
    ag = pt[o:o + a_width]; o += a_width
    bcq = pt[o:o + B_Q_RANK]; o += B_Q_RANK
    bckv = pt[o:o + B_KV_RANK]; o += B_KV_RANK
    bkr = pt[o:o + B_ROPE_DIM]; o += B_ROPE_DIM
    bg = pt[o:o + b_width]

    tab_a = tab_a_ref[...]
    tab_b = tab_b_ref[...]
    ones = _ones_rows(t)

    for h in range(a_heads):
        qh = _rms_rows(aq[h * A_HEAD_DIM:(h + 1) * A_HEAD_DIM], gaq_ref[...])
        qa_ref[0, h] = (_rope_rows(qh, tab_a) * a_scale).astype(BF16)
    for h in range(A_KV_HEADS):
        kh = _rope_rows(_rms_rows(ak[h * A_HEAD_DIM:(h + 1) * A_HEAD_DIM], gak_ref[...]), tab_a)
        kh = jnp.concatenate([kh, jnp.zeros((LANES - A_HEAD_DIM, t), F32)], axis=0)
        ka_ref[0, h] = kh.T.astype(BF16)
        vh = av[h * A_HEAD_DIM:(h + 1) * A_HEAD_DIM]
        va_ref[0, h] = jnp.concatenate([vh, ones], axis=0).astype(BF16)
    ga_ref[0] = (ag * jax.nn.sigmoid(ag)).astype(BF16)

    cq = _rms_rows(bcq, gcq_ref[...]).astype(BF16)
    ckv = _rms_rows(bckv, gckv_ref[...]).astype(BF16)
    qbt = jnp.dot(w_uq_ref[...], cq, preferred_element_type=F32)
    kvt = jnp.dot(w_ukv_ref[...], ckv, preferred_element_type=F32)
    kv_dim = B_NOPE_DIM + B_V_HEAD_DIM
    for h in range(b_heads):
        qh = _rms_rows(qbt[h * B_QK_DIM:(h + 1) * B_QK_DIM], gbq_ref[...])
        qh = jnp.concatenate([qh[:B_NOPE_DIM], _rope_rows(qh[B_NOPE_DIM:], tab_b)], axis=0)
        qb_ref[0, h] = (qh * b_scale).astype(BF16)
        kh = jnp.concatenate([kvt[h * kv_dim:h * kv_dim + B_NOPE_DIM], bkr], axis=0)
        kh = _rms_rows(kh, gbk_ref[...])
        kh = jnp.concatenate([kh[:B_NOPE_DIM], _rope_rows(kh[B_NOPE_DIM:], tab_b),
                              jnp.zeros((LANES - B_QK_DIM, t), F32)], axis=0)
        kb_ref[0, h] = kh.T.astype(BF16)
        vh = kvt[h * kv_dim + B_NOPE_DIM:(h + 1) * kv_dim]
        vb_ref[0, h] = jnp.concatenate([vh, ones], axis=0).astype(BF16)
    gb_ref[0] = (bg * jax.nn.sigmoid(bg)).astype(BF16)


def _attn_kernel(q_ref, k_ref, v_ref, g_ref, o_ref, s_ref, cmax_ref, m_ref, acc_ref, *, dk, dv, tq, tk, ts):
    s_len = k_ref.shape[2]
    nk = s_len // tk
    nt = s_len // tq
    n_strips = tq // ts

    def scores(t, j, buf, c):
        q0 = pl.multiple_of(t * tq + c * ts, ts)
        k0 = pl.multiple_of(j * tk, tk)
        k = k_ref[0, 0, pl.ds(k0, tk), :][:, :dk]
        s = jnp.dot(k, q_ref[0, 0, :, pl.ds(q0, ts)], preferred_element_type=F32).astype(BF16)
        s_ref[buf, :, c * ts:(c + 1) * ts] = s
        cmax_ref[buf, :, c * ts:(c + 1) * ts] = jnp.max(s, axis=0, keepdims=True).astype(F32)

    def update(j, buf, c):
        cols = slice(c * ts, (c + 1) * ts)
        k0 = pl.multiple_of(j * tk, tk)
        vt = v_ref[0, 0, :, pl.ds(k0, tk)]
        m = m_ref[:, cols]
        m_new = jnp.maximum(m, cmax_ref[buf, :, cols])
        p = jnp.exp2(s_ref[buf, :, cols] - m_new.astype(BF16))
        alpha = jnp.exp2(m - m_new)
        m_ref[:, cols] = m_new
        acc_ref[:, cols] = acc_ref[:, cols] * alpha + jnp.dot(vt, p, preferred_element_type=F32)

    def step(t_next, j_next, j_cur, buf_cur):
        for c in range(n_strips):
            scores(t_next, j_next, 1 - buf_cur, c)
            update(j_cur, buf_cur, c)

    def reset():
        m_ref[...] = jnp.full(m_ref.shape, -jnp.inf, F32)
        acc_ref[...] = jnp.zeros(acc_ref.shape, F32)

    def finalize(t):
        q0 = pl.multiple_of(t * tq, tq)
        denom = acc_ref[dv:dv + 1, :]
        y = acc_ref[0:dv, :] * (1.0 / denom)
        o_ref[0, :, pl.ds(q0, tq)] = (y * g_ref[0, :, pl.ds(q0, tq)].astype(F32)).astype(BF16)

    reset()
    for c in range(n_strips):
        scores(0, 0, 0, c)

    def tile_body(t, _):
        def pair(i, _):
            j = 2 * i
            step(t, j + 1, j, 0)
            step(t, j + 2, j + 1, 1)
            return 0

        lax.fori_loop(0, nk // 2 - 1, pair, 0)
        step(t, nk - 1, nk - 2, 0)
        step(jnp.minimum(t + 1, nt - 1), 0, nk - 1, 1)
        finalize(t)
        reset()
        return 0

    lax.fori_loop(0, nt, tile_body, 0)


def _out_kernel(x_ref, ya_ref, yb_ref, wa_ref, wb_ref, o_ref):
    dn = (((0,), (0,)), ((), ()))
    upd = lax.dot_general(ya_ref[0], wa_ref[...], dn, preferred_element_type=F32)
    upd = upd + lax.dot_general(yb_ref[0], wb_ref[...], dn, preferred_element_type=F32)
    o_ref[0] = x_ref[0] + upd


def _attention(qt, k, vt, gt, *, dk, dv, group, tq, ts, tk):
    bsz, n_heads, _, s_len = qt.shape
    n_kv = k.shape[1]
    assert n_heads == n_kv * group and s_len % tq == 0 and tq % ts == 0
    assert s_len % (2 * tk) == 0, "the key-chunk pipeline runs two chunks per step"
    grid = (bsz, n_kv, group)
    return pl.pallas_call(
        functools.partial(_attn_kernel, dk=dk, dv=dv, tq=tq, tk=tk, ts=ts),
        scratch_shapes=[pltpu.VMEM((2, tk, tq), BF16), pltpu.VMEM((2, 1, tq), F32),
                        pltpu.VMEM((1, tq), F32), pltpu.VMEM((dv + BF16_SUBLANES, tq), F32)],
        grid=grid,
        in_specs=[
            pl.BlockSpec((1, 1, dk, s_len), lambda b, kv, g: (b, kv * group + g, 0, 0)),
            pl.BlockSpec((1, 1, s_len, LANES), lambda b, kv, g: (b, kv, 0, 0)),
            pl.BlockSpec((1, 1, dv + BF16_SUBLANES, s_len), lambda b, kv, g: (b, kv, 0, 0)),
            pl.BlockSpec((1, dv, s_len), lambda b, kv, g: (b, kv * group + g, 0)),
        ],
        out_specs=pl.BlockSpec((1, dv, s_len), lambda b, kv, g: (b, kv * group + g, 0)),
        out_shape=jax.ShapeDtypeStruct((bsz, n_heads * dv, s_len), BF16),
        compiler_params=pltpu.CompilerParams(
            dimension_semantics=("arbitrary",) * 3, vmem_limit_bytes=VMEM_LIMIT_BYTES),
    )(qt, k, vt, gt)


def _layer(h, norm_in, w_in, a_q_norm, a_k_norm, b_cq_norm, b_ckv_norm, w_uq, w_ukv, b_q_norm, b_k_norm,
           w_out, tab_a, tab_b):
    bsz, s_len, d_model = h.shape
    n_in = w_in.shape[1]
    b_heads = w_uq.shape[1] // B_QK_DIM
    b_width = b_heads * B_V_HEAD_DIM
    a_width = w_out.shape[0] - b_width
    a_heads = a_width // A_HEAD_DIM
    group = a_heads // A_KV_HEADS
    a_scale = LOG2_E / float(np.sqrt(A_HEAD_DIM))
    b_scale = LOG2_E / float(np.sqrt(B_QK_DIM))

    tp = min(256, s_len)
    tq = min(1024, s_len)
    ts = min(256, s_len)
    tk = min(1024, s_len // 2)
    to = min(256, s_len)

    col = lambda g: g.reshape(-1, 1).astype(F32)
    full = lambda a: pl.BlockSpec(a.shape, lambda b, i: (0,) * a.ndim)
    w_in_t = w_in.T.astype(BF16)
    w_uq_t = w_uq.T.astype(BF16)
    w_ukv_t = w_ukv.T.astype(BF16)
    small = [norm_in.reshape(1, -1).astype(F32), w_in_t, col(a_q_norm), col(a_k_norm), col(b_cq_norm),
             col(b_ckv_norm), w_uq_t, w_ukv_t, col(b_q_norm), col(b_k_norm)]

    va_rows = A_HEAD_DIM + BF16_SUBLANES
    vb_rows = B_V_HEAD_DIM + BF16_SUBLANES
    out_shapes = [
        jax.ShapeDtypeStruct((bsz, a_heads, A_HEAD_DIM, s_len), BF16),
        jax.ShapeDtypeStruct((bsz, A_KV_HEADS, s_len, LANES), BF16),
        jax.ShapeDtypeStruct((bsz, A_KV_HEADS, va_rows, s_len), BF16),
        jax.ShapeDtypeStruct((bsz, a_width, s_len), BF16),
        jax.ShapeDtypeStruct((bsz, b_heads, B_QK_DIM, s_len), BF16),
        jax.ShapeDtypeStruct((bsz, b_heads, s_len, LANES), BF16),
        jax.ShapeDtypeStruct((bsz, b_heads, vb_rows, s_len), BF16),
        jax.ShapeDtypeStruct((bsz, b_width, s_len), BF16),
    ]
    out_specs = [
        pl.BlockSpec((1, a_heads, A_HEAD_DIM, tp), lambda b, i: (b, 0, 0, i)),
        pl.BlockSpec((1, A_KV_HEADS, tp, LANES), lambda b, i: (b, 0, i, 0)),
        pl.BlockSpec((1, A_KV_HEADS, va_rows, tp), lambda b, i: (b, 0, 0, i)),
        pl.BlockSpec((1, a_width, tp), lambda b, i: (b, 0, i)),
        pl.BlockSpec((1, b_heads, B_QK_DIM, tp), lambda b, i: (b, 0, 0, i)),
        pl.BlockSpec((1, b_heads, tp, LANES), lambda b, i: (b, 0, i, 0)),
        pl.BlockSpec((1, b_heads, vb_rows, tp), lambda b, i: (b, 0, 0, i)),
        pl.BlockSpec((1, b_width, tp), lambda b, i: (b, 0, i)),
    ]
    qa, ka, va, ga, qb, kb, vb, gb = pl.pallas_call(
        functools.partial(_proj_kernel, a_heads=a_heads, b_heads=b_heads, a_scale=a_scale, b_scale=b_scale),
        grid=(bsz, s_len // tp),
        in_specs=[pl.BlockSpec((1, tp, d_model), lambda b, i: (b, i, 0))] + [full(a) for a in small] + [
            pl.BlockSpec((4, A_HEAD_DIM // 4, tp), lambda b, i: (0, 0, i)),
            pl.BlockSpec((4, B_ROPE_DIM // 4, tp), lambda b, i: (0, 0, i)),
        ],
        out_specs=out_specs,
        out_shape=out_shapes,
        compiler_params=pltpu.CompilerParams(
            dimension_semantics=("arbitrary", "arbitrary"), vmem_limit_bytes=VMEM_LIMIT_BYTES),
    )(h, *small, tab_a, tab_b)
    assert n_in == w_in_t.shape[0]

    ya = _attention(qa, ka, va, ga, dk=A_HEAD_DIM, dv=A_HEAD_DIM, group=group, tq=tq, ts=ts, tk=tk)
    yb = _attention(qb, kb, vb, gb, dk=B_QK_DIM, dv=B_V_HEAD_DIM, group=1, tq=tq, ts=ts, tk=tk)

    wa = w_out[:a_width].astype(BF16)
    wb = w_out[a_width:].astype(BF16)
    return pl.pallas_call(
        _out_kernel,
        grid=(bsz, s_len // to),
        in_specs=[
            pl.BlockSpec((1, to, d_model), lambda b, i: (b, i, 0)),
            pl.BlockSpec((1, a_width, to), lambda b, i: (b, 0, i)),
            pl.BlockSpec((1, b_width, to), lambda b, i: (b, 0, i)),
            pl.BlockSpec(wa.shape, lambda b, i: (0, 0)),
            pl.BlockSpec(wb.shape, lambda b, i: (0, 0)),
        ],
        out_specs=pl.BlockSpec((1, to, d_model), lambda b, i: (b, i, 0)),
        out_shape=jax.ShapeDtypeStruct(h.shape, h.dtype),
        compiler_params=pltpu.CompilerParams(
            dimension_semantics=("arbitrary", "arbitrary"), vmem_limit_bytes=VMEM_LIMIT_BYTES),
    )(h, ya, yb, wa, wb)


def kernel(x, norm_in, w_in, a_q_norm, a_k_norm, b_cq_norm, b_ckv_norm, w_uq, w_ukv, b_q_norm, b_k_norm, w_out):
    s_len = x.shape[1]
    tab_a = _rope_tables_t(s_len, A_HEAD_DIM)
    tab_b = _rope_tables_t(s_len, B_ROPE_DIM)
    h = x
    for l in range(norm_in.shape[0]):
        h = _layer(h, norm_in[l], w_in[l], a_q_norm[l], a_k_norm[l], b_cq_norm[l], b_ckv_norm[l],
                   w_uq[l], w_ukv[l], b_q_norm[l], b_k_norm[l], w_out[l], tab_a, tab_b)
    return h
```

```python
import functools

import jax
import jax.numpy as jnp
import numpy as np
from jax import lax
from jax.experimental import pallas as pl
from jax.experimental.pallas import tpu as pltpu

GRID_W = 64
ROPE_THETA = 10000.0
EPS = 1e-6
A_HEAD_DIM = 64
A_KV_HEADS = 2
B_V_HEAD_DIM = 128
B_NOPE_DIM = 64
B_ROPE_DIM = 32
B_QK_DIM = B_NOPE_DIM + B_ROPE_DIM
B_Q_RANK = 384
B_KV_RANK = 256

LANES = 128
BF16_SUBLANES = 16
VMEM_LIMIT_BYTES = 48 * 1024 * 1024
LOG2_E = float(np.log2(np.e))

BF16 = jnp.bfloat16
F32 = jnp.float32


def _rope_tables_t(seq_len, dim):
    rows = seq_len // GRID_W
    row = jnp.repeat(jnp.arange(rows, dtype=F32), GRID_W)
    col = jnp.tile(jnp.arange(GRID_W, dtype=F32), rows)
    half = dim // 2
    inv = 1.0 / (ROPE_THETA ** (jnp.arange(0, half, 2, dtype=F32) / half))
    ang_r = inv[:, None] * row[None, :]
    ang_c = inv[:, None] * col[None, :]
    return jnp.stack([jnp.cos(ang_r), jnp.sin(ang_r), jnp.cos(ang_c), jnp.sin(ang_c)])


def _rms_rows(xt, gain_col):
    ms = jnp.mean(xt * xt, axis=0, keepdims=True)
    return xt * lax.rsqrt(ms + EPS) * gain_col


def _rope_rows(xt, tab):
    q = xt.shape[0] // 4
    x1r, x2r, x1c, x2c = xt[0:q], xt[q:2 * q], xt[2 * q:3 * q], xt[3 * q:4 * q]
    cr, sr, cc, sc = tab[0], tab[1], tab[2], tab[3]
    return jnp.concatenate(
        [x1r * cr - x2r * sr, x1r * sr + x2r * cr, x1c * cc - x2c * sc, x1c * sc + x2c * cc], axis=0)


def _ones_rows(t):
    r = lax.broadcasted_iota(jnp.int32, (BF16_SUBLANES, t), 0)
    return jnp.where(r == 0, 1.0, 0.0).astype(F32)


def _proj_kernel(x_ref, gin_ref, w_in_ref, gaq_ref, gak_ref, gcq_ref, gckv_ref, w_uq_ref, w_ukv_ref,
                 gbq_ref, gbk_ref, tab_a_ref, tab_b_ref,
                 qa_ref, ka_ref, va_ref, ga_ref, qb_ref, kb_ref, vb_ref, gb_ref,
                 *, a_heads, b_heads, a_scale, b_scale):
    t = x_ref.shape[1]
    a_width = a_heads * A_HEAD_DIM
    b_width = b_heads * B_V_HEAD_DIM
    x = x_ref[0]
    xn = x * lax.rsqrt(jnp.mean(x * x, axis=-1, keepdims=True) + EPS) * gin_ref[...]
    pt = lax.dot_general(w_in_ref[...], xn.astype(BF16), (((1,), (1,)), ((), ())),
                         preferred_element_type=F32)
    o = 0
    aq = pt[o:o + a_width]; o += a_width
    ak = pt[o:o + A_KV_HEADS * A_HEAD_DIM]; o += A_KV_HEADS * A_HEAD_DIM
    av = pt[o:o + A_KV_HEADS * A_HEAD_DIM]; o += A_KV_HEADS * A_HEAD_DIM
    ag = pt[o:o + a_width]; o += a_width
    bcq = pt[o:o + B_Q_RANK]; o += B_Q_RANK
    bckv = pt[o:o + B_KV_RANK]; o += B_KV_RANK
    bkr = pt[o:o + B_ROPE_DIM]; o += B_ROPE_DIM
    bg = pt[o:o + b_width]

    tab_a = tab_a_ref[...]
    tab_b = tab_b_ref[...]
    ones = _ones_rows(t)

    for h in range(a_heads):
        qh = _rms_rows(aq[h * A_HEAD_DIM:(h + 1) * A_HEAD_DIM], gaq_ref[...])
        qa_ref[0, h] = (_rope_rows(qh, tab_a) * a_scale).astype(BF16)
    for h in range(A_KV_HEADS):
        kh = _rope_rows(_rms_rows(ak[h * A_HEAD_DIM:(h + 1) * A_HEAD_DIM], gak_ref[...]), tab_a)
        kh = jnp.concatenate([kh, jnp.zeros((LANES - A_HEAD_DIM, t), F32)], axis=0)
        ka_ref[0, h] = kh.T.astype(BF16)
        vh = av[h * A_HEAD_DIM:(h + 1) * A_HEAD_DIM]
        va_ref[0, h] = jnp.concatenate([vh, ones], axis=0).astype(BF16)
    ga_ref[0] = (ag * jax.nn.sigmoid(ag)).astype(BF16)

    cq = _rms_rows(bcq, gcq_ref[...]).astype(BF16)
    ckv = _rms_rows(bckv, gckv_ref[...]).astype(BF16)
    qbt = jnp.dot(w_uq_ref[...], cq, preferred_element_type=F32)
    kvt = jnp.dot(w_ukv_ref[...], ckv, preferred_element_type=F32)
    kv_dim = B_NOPE_DIM + B_V_HEAD_DIM
    for h in range(b_heads):
        qh = _rms_rows(qbt[h * B_QK_DIM:(h + 1) * B_QK_DIM], gbq_ref[...])
        qh = jnp.concatenate([qh[:B_NOPE_DIM], _rope_rows(qh[B_NOPE_DIM:], tab_b)], axis=0)
        qb_ref[0, h] = (qh * b_scale).astype(BF16)
        kh = jnp.concatenate([kvt[h * kv_dim:h * kv_dim + B_NOPE_DIM], bkr], axis=0)
        kh = _rms_rows(kh, gbk_ref[...])
        kh = jnp.concatenate([kh[:B_NOPE_DIM], _rope_rows(kh[B_NOPE_DIM:], tab_b),
                              jnp.zeros((LANES - B_QK_DIM, t), F32)], axis=0)
        kb_ref[0, h] = kh.T.astype(BF16)
        vh = kvt[h * kv_dim + B_NOPE_DIM:(h + 1) * kv_dim]
        vb_ref[0, h] = jnp.concatenate([vh, ones], axis=0).astype(BF16)
    gb_ref[0] = (bg * jax.nn.sigmoid(bg)).astype(BF16)


def _attn_kernel(q_ref, k_ref, v_ref, g_ref, o_ref, s_ref, cmax_ref, m_ref, acc_ref, *, dk, dv, tq, tk, ts):
    s_len = k_ref.shape[2]
    nk = s_len // tk
    nt = s_len // tq
    n_strips = tq // ts

    def scores(t, j, buf, c):
        q0 = pl.multiple_of(t * tq + c * ts, ts)
        k0 = pl.multiple_of(j * tk, tk)
        k = k_ref[0, 0, pl.ds(k0, tk), :][:, :dk]
        s = jnp.dot(k, q_ref[0, 0, :, pl.ds(q0, ts)], preferred_element_type=F32).astype(BF16)
        s_ref[buf, :, c * ts:(c + 1) * ts] = s
        cmax_ref[buf, :, c * ts:(c + 1) * ts] = jnp.max(s, axis=0, keepdims=True).astype(F32)

    def update(j, buf, c):
        cols = slice(c * ts, (c + 1) * ts)
        k0 = pl.multiple_of(j * tk, tk)
        vt = v_ref[0, 0, :, pl.ds(k0, tk)]
        m = m_ref[:, cols]
        m_new = jnp.maximum(m, cmax_ref[buf, :, cols])
        p = jnp.exp2(s_ref[buf, :, cols] - m_new.astype(BF16))
        alpha = jnp.exp2(m - m_new)
        m_ref[:, cols] = m_new
        acc_ref[:, cols] = acc_ref[:, cols] * alpha + jnp.dot(vt, p, preferred_element_type=F32)

    def step(t_next, j_next, j_cur, buf_cur):
        for c in range(n_strips):
            scores(t_next, j_next, 1 - buf_cur, c)
            update(j_cur, buf_cur, c)

    def reset():
        m_ref[...] = jnp.full(m_ref.shape, -jnp.inf, F32)
        acc_ref[...] = jnp.zeros(acc_ref.shape, F32)

    def finalize(t):
        q0 = pl.multiple_of(t * tq, tq)
        denom = acc_ref[dv:dv + 1, :]
        y = acc_ref[0:dv, :] * (1.0 / denom)
        o_ref[0, :, pl.ds(q0, tq)] = (y * g_ref[0, :, pl.ds(q0, tq)].astype(F32)).astype(BF16)

    reset()
    for c in range(n_strips):
        scores(0, 0, 0, c)

    def tile_body(t, _):
        for j in range(nk - 1):
            step(t, j + 1, j, j % 2)
        step(jnp.minimum(t + 1, nt - 1), 0, nk - 1, 1)
        finalize(t)
        reset()
        return 0

    lax.fori_loop(0, nt, tile_body, 0)


def _out_kernel(x_ref, ya_ref, yb_ref, wa_ref, wb_ref, o_ref):
    dn = (((0,), (0,)), ((), ()))
    upd = lax.dot_general(ya_ref[0], wa_ref[...], dn, preferred_element_type=F32)
    upd = upd + lax.dot_general(yb_ref[0], wb_ref[...], dn, preferred_element_type=F32)
    o_ref[0] = x_ref[0] + upd


def _attention(qt, k, vt, gt, *, dk, dv, group, tq, ts, tk):
    bsz, n_heads, _, s_len = qt.shape
    n_kv = k.shape[1]
    assert n_heads == n_kv * group and s_len % tq == 0 and tq % ts == 0
    assert s_len % (2 * tk) == 0, "the key-chunk pipeline runs two chunks per step"
    grid = (bsz, n_kv, group)
    return pl.pallas_call(
        functools.partial(_attn_kernel, dk=dk, dv=dv, tq=tq, tk=tk, ts=ts),
        scratch_shapes=[pltpu.VMEM((2, tk, tq), BF16), pltpu.VMEM((2, 1, tq), F32),
                        pltpu.VMEM((1, tq), F32), pltpu.VMEM((dv + BF16_SUBLANES, tq), F32)],
        grid=grid,
        in_specs=[
            pl.BlockSpec((1, 1, dk, s_len), lambda b, kv, g: (b, kv * group + g, 0, 0)),
            pl.BlockSpec((1, 1, s_len, LANES), lambda b, kv, g: (b, kv, 0, 0)),
            pl.BlockSpec((1, 1, dv + BF16_SUBLANES, s_len), lambda b, kv, g: (b, kv, 0, 0)),
            pl.BlockSpec((1, dv, s_len), lambda b, kv, g: (b, kv * group + g, 0)),
        ],
        out_specs=pl.BlockSpec((1, dv, s_len), lambda b, kv, g: (b, kv * group + g, 0)),
        out_shape=jax.ShapeDtypeStruct((bsz, n_heads * dv, s_len), BF16),
        compiler_params=pltpu.CompilerParams(
            dimension_semantics=("arbitrary",) * 3, vmem_limit_bytes=VMEM_LIMIT_BYTES),
    )(qt, k, vt, gt)


def _layer(h, norm_in, w_in, a_q_norm, a_k_norm, b_cq_norm, b_ckv_norm, w_uq, w_ukv, b_q_norm, b_k_norm,
           w_out, tab_a, tab_b):
    bsz, s_len, d_model = h.shape
    n_in = w_in.shape[1]
    b_heads = w_uq.shape[1] // B_QK_DIM
    b_width = b_heads * B_V_HEAD_DIM
    a_width = w_out.shape[0] - b_width
    a_heads = a_width // A_HEAD_DIM
    group = a_heads // A_KV_HEADS
    a_scale = LOG2_E / float(np.sqrt(A_HEAD_DIM))
    b_scale = LOG2_E / float(np.sqrt(B_QK_DIM))

    tp = min(256, s_len)
    tq = min(1024, s_len)
    ts = min(256, s_len)
    tk = min(1024, s_len // 2)
    to = min(256, s_len)

    col = lambda g: g.reshape(-1, 1).astype(F32)
    full = lambda a: pl.BlockSpec(a.shape, lambda b, i: (0,) * a.ndim)
    w_in_t = w_in.T.astype(BF16)
    w_uq_t = w_uq.T.astype(BF16)
    w_ukv_t = w_ukv.T.astype(BF16)
    small = [norm_in.reshape(1, -1).astype(F32), w_in_t, col(a_q_norm), col(a_k_norm), col(b_cq_norm),
             col(b_ckv_norm), w_uq_t, w_ukv_t, col(b_q_norm), col(b_k_norm)]

    va_rows = A_HEAD_DIM + BF16_SUBLANES
    vb_rows = B_V_HEAD_DIM + BF16_SUBLANES
    out_shapes = [
        jax.ShapeDtypeStruct((bsz, a_heads, A_HEAD_DIM, s_len), BF16),
        jax.ShapeDtypeStruct((bsz, A_KV_HEADS, s_len, LANES), BF16),
        jax.ShapeDtypeStruct((bsz, A_KV_HEADS, va_rows, s_len), BF16),
        jax.ShapeDtypeStruct((bsz, a_width, s_len), BF16),
        jax.ShapeDtypeStruct((bsz, b_heads, B_QK_DIM, s_len), BF16),
        jax.ShapeDtypeStruct((bsz, b_heads, s_len, LANES), BF16),
        jax.ShapeDtypeStruct((bsz, b_heads, vb_rows, s_len), BF16),
        jax.ShapeDtypeStruct((bsz, b_width, s_len), BF16),
    ]
    out_specs = [
        pl.BlockSpec((1, a_heads, A_HEAD_DIM, tp), lambda b, i: (b, 0, 0, i)),
        pl.BlockSpec((1, A_KV_HEADS, tp, LANES), lambda b, i: (b, 0, i, 0)),
        pl.BlockSpec((1, A_KV_HEADS, va_rows, tp), lambda b, i: (b, 0, 0, i)),
        pl.BlockSpec((1, a_width, tp), lambda b, i: (b, 0, i)),
        pl.BlockSpec((1, b_heads, B_QK_DIM, tp), lambda b, i: (b, 0, 0, i)),
        pl.BlockSpec((1, b_heads, tp, LANES), lambda b, i: (b, 0, i, 0)),
        pl.BlockSpec((1, b_heads, vb_rows, tp), lambda b, i: (b, 0, 0, i)),
        pl.BlockSpec((1, b_width, tp), lambda b, i: (b, 0, i)),
    ]
    qa, ka, va, ga, qb, kb, vb, gb = pl.pallas_call(
        functools.partial(_proj_kernel, a_heads=a_heads, b_heads=b_heads, a_scale=a_scale, b_scale=b_scale),
        grid=(bsz, s_len // tp),
        in_specs=[pl.BlockSpec((1, tp, d_model), lambda b, i: (b, i, 0))] + [full(a) for a in small] + [
            pl.BlockSpec((4, A_HEAD_DIM // 4, tp), lambda b, i: (0, 0, i)),
            pl.BlockSpec((4, B_ROPE_DIM // 4, tp), lambda b, i: (0, 0, i)),
        ],
        out_specs=out_specs,
        out_shape=out_shapes,
        compiler_params=pltpu.CompilerParams(
            dimension_semantics=("arbitrary", "arbitrary"), vmem_limit_bytes=VMEM_LIMIT_BYTES),
    )(h, *small, tab_a, tab_b)
    assert n_in == w_in_t.shape[0]

    ya = _attention(qa, ka, va, ga, dk=A_HEAD_DIM, dv=A_HEAD_DIM, group=group, tq=tq, ts=ts, tk=tk)
    yb = _attention(qb, kb, vb, gb, dk=B_QK_DIM, dv=B_V_HEAD_DIM, group=1, tq=tq, ts=ts, tk=tk)

    wa = w_out[:a_width].astype(BF16)
    wb = w_out[a_width:].astype(BF16)
    return pl.pallas_call(
        _out_kernel,
        grid=(bsz, s_len // to),
        in_specs=[
            pl.BlockSpec((1, to, d_model), lambda b, i: (b, i, 0)),
            pl.BlockSpec((1, a_width, to), lambda b, i: (b, 0, i)),
            pl.BlockSpec((1, b_width, to), lambda b, i: (b, 0, i)),
            pl.BlockSpec(wa.shape, lambda b, i: (0, 0)),
            pl.BlockSpec(wb.shape, lambda b, i: (0, 0)),
        ],
        out_specs=pl.BlockSpec((1, to, d_model), lambda b, i: (b, i, 0)),
        out_shape=jax.ShapeDtypeStruct(h.shape, h.dtype),
        compiler_params=pltpu.CompilerParams(
            dimension_semantics=("arbitrary", "arbitrary"), vmem_limit_bytes=VMEM_LIMIT_BYTES),
    )(h, ya, yb, wa, wb)


def kernel(x, norm_in, w_in, a_q_norm, a_k_norm, b_cq_norm, b_ckv_norm, w_uq, w_ukv, b_q_norm, b_k_norm, w_out):
    s_len = x.shape[1]
    tab_a = _rope_tables_t(s_len, A_HEAD_DIM)
    tab_b = _rope_tables_t(s_len, B_ROPE_DIM)
    h = x
    for l in range(norm_in.shape[0]):
        h = _layer(h, norm_in[l], w_in[l], a_q_norm[l], a_k_norm[l], b_cq_norm[l], b_ckv_norm[l],
                   w_uq[l], w_ukv[l], b_q_norm[l], b_k_norm[l], w_out[l], tab_a, tab_b)
    return h
```

```python
import functools

import jax
import jax.numpy as jnp
import numpy as np
from jax import lax
from jax.experimental import pallas as pl
from jax.experimental.pallas import tpu as pltpu

GRID_W = 64
ROPE_THETA = 10000.0
EPS = 1e-6
A_HEAD_DIM = 64
A_KV_HEADS = 2
B_V_HEAD_DIM = 128
B_NOPE_DIM = 64
B_ROPE_DIM = 32
B_QK_DIM = B_NOPE_DIM + B_ROPE_DIM
B_Q_RANK = 384
B_KV_RANK = 256

LANES = 128
BF16_SUBLANES = 16
VMEM_LIMIT_BYTES = 48 * 1024 * 1024
LOG2_E = float(np.log2(np.e))

BF16 = jnp.bfloat16
F32 = jnp.float32


def _rope_tables_t(seq_len, dim):
    rows = seq_len // GRID_W
    row = jnp.repeat(jnp.arange(rows, dtype=F32), GRID_W)
    col = jnp.tile(jnp.arange(GRID_W, dtype=F32), rows)
    half = dim // 2
    inv = 1.0 / (ROPE_THETA ** (jnp.arange(0, half, 2, dtype=F32) / half))
    ang_r = inv[:, None] * row[None, :]
    ang_c = inv[:, None] * col[None, :]
    return jnp.stack([jnp.cos(ang_r), jnp.sin(ang_r), jnp.cos(ang_c), jnp.sin(ang_c)])


def _rms_rows(xt, gain_col):
    ms = jnp.mean(xt * xt, axis=0, keepdims=True)
    return xt * lax.rsqrt(ms + EPS) * gain_col


def _rope_rows(xt, tab):
    q = xt.shape[0] // 4
    x1r, x2r, x1c, x2c = xt[0:q], xt[q:2 * q], xt[2 * q:3 * q], xt[3 * q:4 * q]
    cr, sr, cc, sc = tab[0], tab[1], tab[2], tab[3]
    return jnp.concatenate(
        [x1r * cr - x2r * sr, x1r * sr + x2r * cr, x1c * cc - x2c * sc, x1c * sc + x2c * cc], axis=0)


def _ones_rows(t):
    r = lax.broadcasted_iota(jnp.int32, (BF16_SUBLANES, t), 0)
    return jnp.where(r == 0, 1.0, 0.0).astype(F32)


def _proj_kernel(x_ref, gin_ref, w_in_ref, gaq_ref, gak_ref, gcq_ref, gckv_ref, w_uq_ref, w_ukv_ref,
                 gbq_ref, gbk_ref, tab_a_ref, tab_b_ref,
                 qa_ref, ka_ref, va_ref, ga_ref, qb_ref, kb_ref, vb_ref, gb_ref,
                 *, a_heads, b_heads, a_scale, b_scale):
    t = x_ref.shape[1]
    a_width = a_heads * A_HEAD_DIM
    b_width = b_heads * B_V_HEAD_DIM
    x = x_ref[0]
    xn = x * lax.rsqrt(jnp.mean(x * x, axis=-1, keepdims=True) + EPS) * gin_ref[...]
    xnb = xn.astype(BF16)
    sizes = (a_width, A_KV_HEADS * A_HEAD_DIM, A_KV_HEADS * A_HEAD_DIM, a_width,
             B_Q_RANK, B_KV_RANK, B_ROPE_DIM, b_width)
    o_aq, o_ak, o_av, o_ag, o_bcq, o_bckv, o_bkr, o_bg, o_end = np.concatenate([[0], np.cumsum(sizes)]).tolist()

    def proj_rows(lo, hi):
        return lax.dot_general(w_in_ref[lo:hi, :], xnb, (((1,), (1,)), ((), ())), preferred_element_type=F32)

    tab_a = tab_a_ref[...]
    tab_b = tab_b_ref[...]
    ones = _ones_rows(t)

    bcq = proj_rows(o_bcq, o_bckv)
    bckv_kr = proj_rows(o_bckv, o_bg)
    bckv, bkr = bckv_kr[:B_KV_RANK], bckv_kr[B_KV_RANK:]

    aq = proj_rows(o_aq, o_ak)
    for h in range(a_heads):
        qh = _rms_rows(aq[h * A_HEAD_DIM:(h + 1) * A_HEAD_DIM], gaq_ref[...])
        qa_ref[0, h] = (_rope_rows(qh, tab_a) * a_scale).astype(BF16)

    cq = _rms_rows(bcq, gcq_ref[...]).astype(BF16)
    ckv = _rms_rows(bckv, gckv_ref[...]).astype(BF16)
    qbt = jnp.dot(w_uq_ref[...], cq, preferred_element_type=F32)
    kvt = jnp.dot(w_ukv_ref[...], ckv, preferred_element_type=F32)

    akv = proj_rows(o_ak, o_ag)
    ak, av = akv[:A_KV_HEADS * A_HEAD_DIM], akv[A_KV_HEADS * A_HEAD_DIM:]
    for h in range(A_KV_HEADS):
        kh = _rope_rows(_rms_rows(ak[h * A_HEAD_DIM:(h + 1) * A_HEAD_DIM], gak_ref[...]), tab_a)
        kh = jnp.concatenate([kh, jnp.zeros((LANES - A_HEAD_DIM, t), F32)], axis=0)
        ka_ref[0, h] = kh.T.astype(BF16)
        vh = av[h * A_HEAD_DIM:(h + 1) * A_HEAD_DIM]
        va_ref[0, h] = jnp.concatenate([vh, ones], axis=0).astype(BF16)
    ag = proj_rows(o_ag, o_bcq)
    ga_ref[0] = (ag * jax.nn.sigmoid(ag)).astype(BF16)
    bg = proj_rows(o_bg, o_end)

    kv_dim = B_NOPE_DIM + B_V_HEAD_DIM
    for h in range(b_heads):
        qh = _rms_rows(qbt[h * B_QK_DIM:(h + 1) * B_QK_DIM], gbq_ref[...])
        qh = jnp.concatenate([qh[:B_NOPE_DIM], _rope_rows(qh[B_NOPE_DIM:], tab_b)], axis=0)
        qb_ref[0, h] = (qh * b_scale).astype(BF16)
        kh = jnp.concatenate([kvt[h * kv_dim:h * kv_dim + B_NOPE_DIM], bkr], axis=0)
        kh = _rms_rows(kh, gbk_ref[...])
        kh = jnp.concatenate([kh[:B_NOPE_DIM], _rope_rows(kh[B_NOPE_DIM:], tab_b),
                              jnp.zeros((LANES - B_QK_DIM, t), F32)], axis=0)
        kb_ref[0, h] = kh.T.astype(BF16)
        vh = kvt[h * kv_dim + B_NOPE_DIM:(h + 1) * kv_dim]
        vb_ref[0, h] = jnp.concatenate([vh, ones], axis=0).astype(BF16)
    gb_ref[0] = (bg * jax.nn.sigmoid(bg)).astype(BF16)


def _attn_kernel(q_ref, k_ref, v_ref, g_ref, o_ref, s_ref, cmax_ref, m_ref, acc_ref, *, dk, dv, tq, tk, ts):
    s_len = k_ref.shape[2]
    nk = s_len // tk
    nt = s_len // tq
    n_strips = tq // ts

    def scores(t, j, buf, c):
        q0 = pl.multiple_of(t * tq + c * ts, ts)
        k0 = pl.multiple_of(j * tk, tk)
        k = k_ref[0, 0, pl.ds(k0, tk), :][:, :dk]
        s = jnp.dot(k, q_ref[0, 0, :, pl.ds(q0, ts)], preferred_element_type=F32).astype(BF16)
        s_ref[buf, :, c * ts:(c + 1) * ts] = s
        cmax_ref[buf, :, c * ts:(c + 1) * ts] = jnp.max(s, axis=0, keepdims=True).astype(F32)

    def update(j, buf, c):
        cols = slice(c * ts, (c + 1) * ts)
        k0 = pl.multiple_of(j * tk, tk)
        vt = v_ref[0, 0, :, pl.ds(k0, tk)]
        m = m_ref[:, cols]
        m_new = jnp.maximum(m, cmax_ref[buf, :, cols])
        p = jnp.exp2(s_ref[buf, :, cols] - m_new.astype(BF16))
        alpha = jnp.exp2(m - m_new)
        m_ref[:, cols] = m_new
        acc_ref[:, cols] = acc_ref[:, cols] * alpha + jnp.dot(vt, p, preferred_element_type=F32)

    def step(t_next, j_next, j_cur, buf_cur):
        for c in range(n_strips):
            scores(t_next, j_next, 1 - buf_cur, c)
            update(j_cur, buf_cur, c)

    def reset():
        m_ref[...] = jnp.full(m_ref.shape, -jnp.inf, F32)
        acc_ref[...] = jnp.zeros(acc_ref.shape, F32)

    def finalize(t):
        q0 = pl.multiple_of(t * tq, tq)
        denom = acc_ref[dv:dv + 1, :]
        y = acc_ref[0:dv, :] * (1.0 / denom)
        o_ref[0, :, pl.ds(q0, tq)] = (y * g_ref[0, :, pl.ds(q0, tq)].astype(F32)).astype(BF16)

    reset()
    for c in range(n_strips):
        scores(0, 0, 0, c)

    def tile_body(t, _):
        for j in range(nk - 1):
            step(t, j + 1, j, j % 2)
        step(jnp.minimum(t + 1, nt - 1), 0, nk - 1, 1)
        finalize(t)
        reset()
        return 0

    lax.fori_loop(0, nt, tile_body, 0)


def _out_kernel(x_ref, ya_ref, yb_ref, wa_ref, wb_ref, o_ref):
    dn = (((0,), (0,)), ((), ()))
    upd = lax.dot_general(ya_ref[0], wa_ref[...], dn, preferred_element_type=F32)
    upd = upd + lax.dot_general(yb_ref[0], wb_ref[...], dn, preferred_element_type=F32)
    o_ref[0] = x_ref[0] + upd


def _attention(qt, k, vt, gt, *, dk, dv, group, tq, ts, tk):
    bsz, n_heads, _, s_len = qt.shape
    n_kv = k.shape[1]
    assert n_heads == n_kv * group and s_len % tq == 0 and tq % ts == 0
    assert s_len % (2 * tk) == 0, "the key-chunk pipeline runs two chunks per step"
    grid = (bsz, n_kv, group)
    return pl.pallas_call(
        functools.partial(_attn_kernel, dk=dk, dv=dv, tq=tq, tk=tk, ts=ts),
        scratch_shapes=[pltpu.VMEM((2, tk, tq), BF16), pltpu.VMEM((2, 1, tq), F32),
                        pltpu.VMEM((1, tq), F32), pltpu.VMEM((dv + BF16_SUBLANES, tq), F32)],
        grid=grid,
        in_specs=[
            pl.BlockSpec((1, 1, dk, s_len), lambda b, kv, g: (b, kv * group + g, 0, 0)),
            pl.BlockSpec((1, 1, s_len, LANES), lambda b, kv, g: (b, kv, 0, 0)),
            pl.BlockSpec((1, 1, dv + BF16_SUBLANES, s_len), lambda b, kv, g: (b, kv, 0, 0)),
            pl.BlockSpec((1, dv, s_len), lambda b, kv, g: (b, kv * group + g, 0)),
        ],
        out_specs=pl.BlockSpec((1, dv, s_len), lambda b, kv, g: (b, kv * group + g, 0)),
        out_shape=jax.ShapeDtypeStruct((bsz, n_heads * dv, s_len), BF16),
        compiler_params=pltpu.CompilerParams(
            dimension_semantics=("arbitrary",) * 3, vmem_limit_bytes=VMEM_LIMIT_BYTES),
    )(qt, k, vt, gt)


def _layer(h, norm_in, w_in, a_q_norm, a_k_norm, b_cq_norm, b_ckv_norm, w_uq, w_ukv, b_q_norm, b_k_norm,
           w_out, tab_a, tab_b):
    bsz, s_len, d_model = h.shape
    n_in = w_in.shape[1]
    b_heads = w_uq.shape[1] // B_QK_DIM
    b_width = b_heads * B_V_HEAD_DIM
    a_width = w_out.shape[0] - b_width
    a_heads = a_width // A_HEAD_DIM
    group = a_heads // A_KV_HEADS
    a_scale = LOG2_E / float(np.sqrt(A_HEAD_DIM))
    b_scale = LOG2_E / float(np.sqrt(B_QK_DIM))

    tp = min(512, s_len)
    tq = min(1024, s_len)
    ts = min(256, s_len)
    tk = min(1024, s_len // 2)
    to = min(512, s_len)

    col = lambda g: g.reshape(-1, 1).astype(F32)
    full = lambda a: pl.BlockSpec(a.shape, lambda b, i: (0,) * a.ndim)
    w_in_t = w_in.T.astype(BF16)
    w_uq_t = w_uq.T.astype(BF16)
    w_ukv_t = w_ukv.T.astype(BF16)
    small = [norm_in.reshape(1, -1).astype(F32), w_in_t, col(a_q_norm), col(a_k_norm), col(b_cq_norm),
             col(b_ckv_norm), w_uq_t, w_ukv_t, col(b_q_norm), col(b_k_norm)]

    va_rows = A_HEAD_DIM + BF16_SUBLANES
    vb_rows = B_V_HEAD_DIM + BF16_SUBLANES
    out_shapes = [
        jax.ShapeDtypeStruct((bsz, a_heads, A_HEAD_DIM, s_len), BF16),
        jax.ShapeDtypeStruct((bsz, A_KV_HEADS, s_len, LANES), BF16),
        jax.ShapeDtypeStruct((bsz, A_KV_HEADS, va_rows, s_len), BF16),
        jax.ShapeDtypeStruct((bsz, a_width, s_len), BF16),
        jax.ShapeDtypeStruct((bsz, b_heads, B_QK_DIM, s_len), BF16),
        jax.ShapeDtypeStruct((bsz, b_heads, s_len, LANES), BF16),
        jax.ShapeDtypeStruct((bsz, b_heads, vb_rows, s_len), BF16),
        jax.ShapeDtypeStruct((bsz, b_width, s_len), BF16),
    ]
    out_specs = [
        pl.BlockSpec((1, a_heads, A_HEAD_DIM, tp), lambda b, i: (b, 0, 0, i)),
        pl.BlockSpec((1, A_KV_HEADS, tp, LANES), lambda b, i: (b, 0, i, 0)),
        pl.BlockSpec((1, A_KV_HEADS, va_rows, tp), lambda b, i: (b, 0, 0, i)),
        pl.BlockSpec((1, a_width, tp), lambda b, i: (b, 0, i)),
        pl.BlockSpec((1, b_heads, B_QK_DIM, tp), lambda b, i: (b, 0, 0, i)),
        pl.BlockSpec((1, b_heads, tp, LANES), lambda b, i: (b, 0, i, 0)),
        pl.BlockSpec((1, b_heads, vb_rows, tp), lambda b, i: (b, 0, 0, i)),
        pl.BlockSpec((1, b_width, tp), lambda b, i: (b, 0, i)),
    ]
    qa, ka, va, ga, qb, kb, vb, gb = pl.pallas_call(
        functools.partial(_proj_kernel, a_heads=a_heads, b_heads=b_heads, a_scale=a_scale, b_scale=b_scale),
        grid=(bsz, s_len // tp),
        in_specs=[pl.BlockSpec((1, tp, d_model), lambda b, i: (b, i, 0))] + [full(a) for a in small] + [
            pl.BlockSpec((4, A_HEAD_DIM // 4, tp), lambda b, i: (0, 0, i)),
            pl.BlockSpec((4, B_ROPE_DIM // 4, tp), lambda b, i: (0, 0, i)),
        ],
        out_specs=out_specs,
        out_shape=out_shapes,
        compiler_params=pltpu.CompilerParams(
            dimension_semantics=("arbitrary", "arbitrary"), vmem_limit_bytes=VMEM_LIMIT_BYTES),
    )(h, *small, tab_a, tab_b)
    assert n_in == w_in_t.shape[0]

    ya = _attention(qa, ka, va, ga, dk=A_HEAD_DIM, dv=A_HEAD_DIM, group=group, tq=tq, ts=ts, tk=tk)
    yb = _attention(qb, kb, vb, gb, dk=B_QK_DIM, dv=B_V_HEAD_DIM, group=1, tq=tq, ts=ts, tk=tk)

    wa = w_out[:a_width].astype(BF16)
    wb = w_out[a_width:].astype(BF16)
    return pl.pallas_call(
        _out_kernel,
        grid=(bsz, s_len // to),
        in_specs=[
            pl.BlockSpec((1, to, d_model), lambda b, i: (b, i, 0)),
            pl.BlockSpec((1, a_width, to), lambda b, i: (b, 0, i)),
            pl.BlockSpec((1, b_width, to), lambda b, i: (b, 0, i)),
            pl.BlockSpec(wa.shape, lambda b, i: (0, 0)),
            pl.BlockSpec(wb.shape, lambda b, i: (0, 0)),
        ],
        out_specs=pl.BlockSpec((1, to, d_model), lambda b, i: (b, i, 0)),
        out_shape=jax.ShapeDtypeStruct(h.shape, h.dtype),
        compiler_params=pltpu.CompilerParams(
            dimension_semantics=("arbitrary", "arbitrary"), vmem_limit_bytes=VMEM_LIMIT_BYTES),
    )(h, ya, yb, wa, wb)


def kernel(x, norm_in, w_in, a_q_norm, a_k_norm, b_cq_norm, b_ckv_norm, w_uq, w_ukv, b_q_norm, b_k_norm, w_out):
    s_len = x.shape[1]
    tab_a = _rope_tables_t(s_len, A_HEAD_DIM)
    tab_b = _rope_tables_t(s_len, B_ROPE_DIM)
    h = x
    for l in range(norm_in.shape[0]):
        h = _layer(h, norm_in[l], w_in[l], a_q_norm[l], a_k_norm[l], b_cq_norm[l], b_ckv_norm[l],
                   w_uq[l], w_ukv[l], b_q_norm[l], b_k_norm[l], w_out[l], tab_a, tab_b)
    return h
```
